```python
import math
import jax
import jax.numpy as jnp
from jax import lax
import numpy as np

D_MODEL = 1024
BATCH = 4
SEQ = 8192
DEPTH = 2

N_META = 16
BLOCK = 128
PREFIX = BLOCK
PAD_LEN = PREFIX - N_META
ROPE_THETA = 10000.0
EPS = 1e-6
L2_EPS = 1e-6
NEG_INF = -1e30

A_HEADS = 4
A_KV_HEADS = 2
A_HEAD_DIM = 64
WINDOW = 128
B_HEADS = 4
B_HEAD_DIM = 128
CONV_WIDTH = 4
CHUNK = 64
C_HEADS = 4
C_HEAD_DIM = 32

A_WIDTH = A_HEADS * A_HEAD_DIM
B_WIDTH = B_HEADS * B_HEAD_DIM
C_WIDTH = C_HEADS * 2 * C_HEAD_DIM
MIX_WIDTH = A_WIDTH + B_WIDTH + C_WIDTH

IN_SECTIONS = (A_WIDTH, A_KV_HEADS * A_HEAD_DIM, A_KV_HEADS * A_HEAD_DIM, 3 * B_WIDTH, B_WIDTH, B_HEADS, B_HEADS, C_WIDTH, C_WIDTH, C_WIDTH)
IN_COLS = sum(IN_SECTIONS)

D_FF = 2816
N_EXPERTS = 8
TOP_K = 2
D_EXPERT = 3584

kernel_name = "hybrid_swa_gdn_diffattn_moe_trunk"


def rmsnorm(x, w):
    xf = x.astype(jnp.float32)
    y = xf * lax.rsqrt(jnp.mean(xf * xf, axis=-1, keepdims=True) + EPS)
    return (y * w.astype(jnp.float32)).astype(x.dtype)


def l2norm(x):
    return x * lax.rsqrt(jnp.sum(x * x, axis=-1, keepdims=True) + L2_EPS)


def apply_rope(x, pos):
    d = x.shape[-1]
    half = d // 2
    inv_freq = ROPE_THETA ** (-jnp.arange(half, dtype=jnp.float32) * (2.0 / d))
    ang = pos.astype(jnp.float32)[:, None] * inv_freq[None, :]
    bshape = (pos.shape[0],) + (1,) * (x.ndim - 3) + (half,)
    cos = jnp.cos(ang).reshape(bshape)
    sin = jnp.sin(ang).reshape(bshape)
    xf = x.astype(jnp.float32)
    x1, x2 = xf[..., :half], xf[..., half:]
    return jnp.concatenate([x1 * cos - x2 * sin, x2 * cos + x1 * sin], axis=-1).astype(x.dtype)


def causal_depthwise_conv(x, w):
    width, ch = w.shape
    return lax.conv_general_dilated(x, w[:, None, :].astype(x.dtype), window_strides=(1,), padding=[(width - 1, 0)], dimension_numbers=("NWC", "WIO", "NWC"), feature_group_count=ch)


def sliding_window_sink_attention(q, k, v, sinks):
    bsz, seqlen, hq, d = q.shape
    hkv = k.shape[2]
    grp = hq // hkv
    nb = seqlen // BLOCK
    scale = d ** -0.5
    qb = q.reshape(bsz, nb, BLOCK, hkv, grp, d)
    k_meta, v_meta = k[:, PAD_LEN:PREFIX], v[:, PAD_LEN:PREFIX]

    def band(t):
        t = t.reshape(bsz, nb, BLOCK, hkv, d)
        prev = jnp.concatenate([jnp.zeros_like(t[:, :1]), t[:, :-1]], axis=1)
        return jnp.concatenate([prev, t], axis=2)

    kb, vb = band(k), band(v)
    s_meta = jnp.einsum("bnqhgd,bmhd->bnhgqm", qb, k_meta, preferred_element_type=jnp.float32) * scale
    s_band = jnp.einsum("bnqhgd,bnkhd->bnhgqk", qb, kb, preferred_element_type=jnp.float32) * scale
    blk = jnp.arange(nb)[:, None] * BLOCK
    qpos = blk + jnp.arange(BLOCK)[None, :]
    kpos = blk - BLOCK + jnp.arange(2 * BLOCK)[None, :]
    mpos = PAD_LEN + jnp.arange(N_META)
    dist = qpos[:, :, None] - kpos[:, None, :]
    band_ok = (dist >= 0) & (dist < WINDOW) & (kpos[:, None, :] >= PREFIX)
    meta_ok = mpos[None, None, :] <= qpos[:, :, None]
    s_meta = jnp.where(meta_ok[None, :, None, None], s_meta, NEG_INF)
    s_band = jnp.where(band_ok[None, :, None, None], s_band, NEG_INF)
    sink = jnp.broadcast_to(sinks.astype(jnp.float32).reshape(1, 1, hkv, grp, 1, 1), s_band.shape[:-1] + (1,))
    p = jax.nn.softmax(jnp.concatenate([s_meta, s_band, sink], axis=-1), axis=-1).astype(v.dtype)
    o = jnp.einsum("bnhgqm,bmhd->bnqhgd", p[..., :N_META], v_meta) + jnp.einsum("bnhgqk,bnkhd->bnqhgd", p[..., N_META:N_META + 2 * BLOCK], vb)
    return o.reshape(bsz, seqlen, hq * d)


def gated_delta_rule(q, k, v, g, beta):
    bsz, seqlen, h, dk = k.shape
    dv = v.shape[-1]
    nc = seqlen // CHUNK
    q = l2norm(q) * (dk ** -0.5)
    k = l2norm(k)

    def chunks(t):
        return jnp.moveaxis(t.reshape(bsz, nc, CHUNK, h, -1), 3, 1)

    qc, kc, vc = chunks(q), chunks(k), chunks(v)
    bc = chunks(beta[..., None])[..., 0]
    gc = jnp.cumsum(chunks(g[..., None])[..., 0], axis=-1)
    incl = jnp.tril(jnp.ones((CHUNK, CHUNK), dtype=bool))
    strict = jnp.tril(jnp.ones((CHUNK, CHUNK), dtype=bool), k=-1)
    decay = jnp.exp(jnp.where(incl, gc[..., :, None] - gc[..., None, :], -jnp.inf))
    k_beta = kc * bc[..., None]
    v_beta = vc * bc[..., None]
    a_mat = jnp.where(strict, jnp.einsum("bhncd,bhnsd->bhncs", k_beta, kc) * decay, 0.0)
    rhs = jnp.concatenate([v_beta, k_beta * jnp.exp(gc)[..., None]], axis=-1)
    sol = lax.linalg.triangular_solve(a_mat, rhs, left_side=True, lower=True, unit_diagonal=True)
    u, w = sol[..., :dv], sol[..., dv:]
    qk = jnp.where(incl, jnp.einsum("bhncd,bhnsd->bhncs", qc, kc) * decay, 0.0)
    q_dec = qc * jnp.exp(gc)[..., None]
    k_dec = kc * jnp.exp(gc[..., -1:] - gc)[..., None]
    g_last = jnp.exp(gc[..., -1])

    def step(state, xs):
        q_i, k_i, u_i, w_i, qk_i, gl_i = xs
        v_new = u_i - jnp.einsum("bhcd,bhde->bhce", w_i, state)
        o_i = jnp.einsum("bhcd,bhde->bhce", q_i, state) + jnp.einsum("bhcs,bhse->bhce", qk_i, v_new)
        state = state * gl_i[..., None, None] + jnp.einsum("bhcd,bhce->bhde", k_i, v_new)
        return state, o_i

    xs = tuple(jnp.moveaxis(t, 2, 0) for t in (q_dec, k_dec, u, w, qk, g_last))
    state0 = jnp.zeros((bsz, h, dk, dv), jnp.float32)
    _, o = lax.scan(step, state0, xs)
    return jnp.transpose(o, (1, 0, 3, 2, 4)).reshape(bsz, seqlen, h, dv)


def differential_attention(q1, q2, k1, k2, v, lam):
    bsz, seqlen, h, d = q1.shape
    nb = seqlen // BLOCK
    scale = d ** -0.5
    kpos = jnp.arange(seqlen)
    key_ok = kpos >= PAD_LEN

    def to_blocks(t):
        return jnp.swapaxes(t.reshape(bsz, nb, BLOCK, h, d), 0, 1)

    def one_block(args):
        n, qa, qb = args
        qpos = n * BLOCK + jnp.arange(BLOCK)
        mask = (kpos[None, :] <= qpos[:, None]) & key_ok[None, :]
        s1 = jnp.einsum("bqhd,bkhd->bhqk", qa, k1, preferred_element_type=jnp.float32) * scale
        s2 = jnp.einsum("bqhd,bkhd->bhqk", qb, k2, preferred_element_type=jnp.float32) * scale
        p1 = jax.nn.softmax(jnp.where(mask, s1, NEG_INF), axis=-1)
        p2 = jax.nn.softmax(jnp.where(mask, s2, NEG_INF), axis=-1)
        p = p1 - lam * p2
        return jnp.einsum("bhqk,bkhe->bqhe", p.astype(v.dtype), v)

    o = lax.map(one_block, (jnp.arange(nb), to_blocks(q1), to_blocks(q2)))
    return jnp.swapaxes(o, 0, 1).reshape(bsz, seqlen, h, 2 * d)


def mixer_layer(h, pos, layer, w_in, a_q_norm, a_k_norm, a_sinks, b_conv, b_a_log, b_dt_bias, b_o_norm, c_q_norm, c_k_norm, c_lambda, c_subln, w_out):
    bsz, seqlen, _ = h.shape
    f32 = jnp.float32
    proj = h @ w_in
    a_q, a_k, a_v, b_qkv, b_z, b_b, b_a, c_q, c_k, c_v = jnp.split(proj, np.cumsum(IN_SECTIONS)[:-1].tolist(), axis=-1)

    qa = apply_rope(rmsnorm(a_q.reshape(bsz, seqlen, A_HEADS, A_HEAD_DIM), a_q_norm), pos)
    ka = apply_rope(rmsnorm(a_k.reshape(bsz, seqlen, A_KV_HEADS, A_HEAD_DIM), a_k_norm), pos)
    va = a_v.reshape(bsz, seqlen, A_KV_HEADS, A_HEAD_DIM)
    out_a = sliding_window_sink_attention(qa, ka, va, a_sinks)

    qkv = jax.nn.silu(causal_depthwise_conv(b_qkv, b_conv)).astype(f32)
    qb, kb, vb = (t.reshape(bsz, seqlen, B_HEADS, B_HEAD_DIM) for t in jnp.split(qkv, 3, axis=-1))
    beta = jax.nn.sigmoid(b_b.astype(f32))
    g = -jnp.exp(b_a_log.astype(f32)) * jax.nn.softplus(b_a.astype(f32) + b_dt_bias.astype(f32))
    ob = gated_delta_rule(qb, kb, vb, g, beta)
    z = b_z.reshape(bsz, seqlen, B_HEADS, B_HEAD_DIM).astype(f32)
    out_b = (rmsnorm(ob, b_o_norm) * jax.nn.silu(z)).reshape(bsz, seqlen, B_WIDTH).astype(h.dtype)

    lambda_init = 0.8 - 0.6 * math.exp(-0.3 * layer)
    lp = c_lambda.astype(f32)
    lam = jnp.exp(jnp.sum(lp[0] * lp[1])) - jnp.exp(jnp.sum(lp[2] * lp[3])) + lambda_init
    cq = apply_rope(rmsnorm(c_q.reshape(bsz, seqlen, C_HEADS, 2, C_HEAD_DIM), c_q_norm), pos)
    ck = apply_rope(rmsnorm(c_k.reshape(bsz, seqlen, C_HEADS, 2, C_HEAD_DIM), c_k_norm), pos)
    vc = c_v.reshape(bsz, seqlen, C_HEADS, 2 * C_HEAD_DIM)
    oc = differential_attention(cq[..., 0, :], cq[..., 1, :], ck[..., 0, :], ck[..., 1, :], vc, lam)
    out_c = (rmsnorm(oc, c_subln) * (1.0 - lambda_init)).reshape(bsz, seqlen, C_WIDTH).astype(h.dtype)

    return jnp.concatenate([out_a, out_b, out_c], axis=-1) @ w_out


def swiglu(h, w_gate, w_up, w_down):
    return (jax.nn.silu(h @ w_gate) * (h @ w_up)) @ w_down


def moe_swiglu(h, router, w_gate, w_up, w_down):
    logits = jnp.einsum("bld,de->ble", h, router, preferred_element_type=jnp.float32)
    top_logits, top_idx = lax.top_k(logits, TOP_K)
    top_w = jax.nn.softmax(top_logits, axis=-1)
    gate = jnp.sum(jax.nn.one_hot(top_idx, N_EXPERTS, dtype=jnp.float32) * top_w[..., None], axis=-2)
    out = jnp.zeros_like(h)
    for e in range(N_EXPERTS):
        out = out + gate[..., e:e + 1].astype(h.dtype) * swiglu(h, w_gate[e], w_up[e], w_down[e])
    return out


def setup_inputs(seed: int = 0) -> dict:
    key = jax.random.key(seed)
    ks = list(jax.random.split(key, 32))
    f32 = jnp.float32
    n_dense = (DEPTH + 1) // 2
    n_moe = DEPTH // 2

    def nrm(i, shape, scale):
        return jax.random.normal(ks[i], shape, f32) * scale

    def gain(i, shape):
        return 1.0 + nrm(i, shape, 0.02)

    dt = jnp.exp(jax.random.uniform(ks[9], (DEPTH, B_HEADS), f32, math.log(1e-3), math.log(1e-1)))
    return {
        "x": nrm(0, (BATCH, SEQ, D_MODEL), 1.0),
        "meta_tokens": nrm(1, (N_META, D_MODEL), 1.0),
        "mix_norm": gain(2, (DEPTH, D_MODEL)),
        "w_in": nrm(3, (DEPTH, D_MODEL, IN_COLS), D_MODEL ** -0.5),
        "a_q_norm": gain(4, (DEPTH, A_HEAD_DIM)),
        "a_k_norm": gain(5, (DEPTH, A_HEAD_DIM)),
        "a_sinks": nrm(6, (DEPTH, A_HEADS), 1.0),
        "b_conv": nrm(7, (DEPTH, CONV_WIDTH, 3 * B_WIDTH), CONV_WIDTH ** -0.5),
        "b_a_log": jnp.log(jax.random.uniform(ks[8], (DEPTH, B_HEADS), f32, 1.0, 16.0)),
        "b_dt_bias": dt + jnp.log(-jnp.expm1(-dt)),
        "b_o_norm": gain(10, (DEPTH, B_HEAD_DIM)),
        "c_q_norm": gain(11, (DEPTH, C_HEAD_DIM)),
        "c_k_norm": gain(12, (DEPTH, C_HEAD_DIM)),
        "c_lambda": nrm(13, (DEPTH, 4, C_HEAD_DIM), 0.1),
        "c_subln": gain(14, (DEPTH, 2 * C_HEAD_DIM)),
        "w_out": nrm(15, (DEPTH, MIX_WIDTH, D_MODEL), MIX_WIDTH ** -0.5),
        "ffn_norm": gain(16, (DEPTH, D_MODEL)),
        "dense_w_gate": nrm(17, (n_dense, D_MODEL, D_FF), D_MODEL ** -0.5),
        "dense_w_up": nrm(18, (n_dense, D_MODEL, D_FF), D_MODEL ** -0.5),
        "dense_w_down": nrm(19, (n_dense, D_FF, D_MODEL), D_FF ** -0.5),
        "router": nrm(20, (n_moe, D_MODEL, N_EXPERTS), D_MODEL ** -0.5),
        "moe_w_gate": nrm(21, (n_moe, N_EXPERTS, D_MODEL, D_EXPERT), D_MODEL ** -0.5),
        "moe_w_up": nrm(22, (n_moe, N_EXPERTS, D_MODEL, D_EXPERT), D_MODEL ** -0.5),
        "moe_w_down": nrm(23, (n_moe, N_EXPERTS, D_EXPERT, D_MODEL), D_EXPERT ** -0.5),
    }


def reference(x, meta_tokens, mix_norm, w_in, a_q_norm, a_k_norm, a_sinks, b_conv, b_a_log, b_dt_bias, b_o_norm, c_q_norm, c_k_norm, c_lambda, c_subln, w_out, ffn_norm, dense_w_gate, dense_w_up, dense_w_down, router, moe_w_gate, moe_w_up, moe_w_down):
    bsz, seq, dm = x.shape
    total = PREFIX + seq
    pad = jnp.zeros((bsz, PAD_LEN, dm), x.dtype)
    meta = jnp.broadcast_to(meta_tokens.astype(x.dtype)[None], (bsz, N_META, dm))
    hs = jnp.concatenate([pad, meta, x], axis=1)
    idx = jnp.arange(total)
    pos = idx - PAD_LEN
    valid = (idx >= PAD_LEN).astype(x.dtype)[None, :, None]
    for layer in range(DEPTH):
        h = rmsnorm(hs, mix_norm[layer]) * valid
        hs = hs + mixer_layer(h, pos, layer, w_in[layer], a_q_norm[layer], a_k_norm[layer], a_sinks[layer], b_conv[layer], b_a_log[layer], b_dt_bias[layer], b_o_norm[layer], c_q_norm[layer], c_k_norm[layer], c_lambda[layer], c_subln[layer], w_out[layer])
        h = rmsnorm(hs, ffn_norm[layer])
        j = layer // 2
        if layer % 2 == 0:
            hs = hs + swiglu(h, dense_w_gate[j], dense_w_up[j], dense_w_down[j])
        else:
            hs = hs + moe_swiglu(h, router[j], moe_w_gate[j], moe_w_up[j], moe_w_down[j])
    return hs[:, PREFIX:]
```

```python
import functools
import math

import jax
import jax.numpy as jnp
from jax import lax
from jax.experimental import pallas as pl
from jax.experimental.pallas import tpu as pltpu

F32 = jnp.float32
BF16 = jnp.bfloat16

D_MODEL = 1024
N_META = 16
BLOCK = 128
PREFIX = BLOCK
PAD_LEN = PREFIX - N_META
ROPE_THETA = 10000.0
EPS = 1e-6
L2_EPS = 1e-6
NEG_INF = -1e30

A_HEADS = 4
A_KV_HEADS = 2
A_HEAD_DIM = 64
WINDOW = 128
B_HEADS = 4
B_HEAD_DIM = 128
CONV_WIDTH = 4
CHUNK = 64
C_HEADS = 4
C_HEAD_DIM = 32

A_WIDTH = A_HEADS * A_HEAD_DIM
B_WIDTH = B_HEADS * B_HEAD_DIM
C_WIDTH = C_HEADS * 2 * C_HEAD_DIM
N_EXPERTS = 8
TOP_K = 2

LANES = 128
VMEM_LIMIT = 56 * 1024 * 1024


def _cparams(n_axes):
    return pltpu.CompilerParams(dimension_semantics=("arbitrary",) * n_axes, vmem_limit_bytes=VMEM_LIMIT)


def _dot(a, b):
    return jnp.dot(a, b, preferred_element_type=F32)


def _dot_nt(a, b):
    return lax.dot_general(a, b, (((1,), (1,)), ((), ())), preferred_element_type=F32)


def _dot_tn(a, b):
    return lax.dot_general(a, b, (((0,), (0,)), ((), ())), preferred_element_type=F32)


def _split3(x):
    hi = x.astype(BF16)
    r = x - hi.astype(F32)
    mid = r.astype(BF16)
    lo = (r - mid.astype(F32)).astype(BF16)
    return hi, mid, lo


def _group_mean(x2, bd):
    hi = x2.astype(BF16)
    lo = (x2 - hi.astype(F32)).astype(BF16)
    return _dot(hi, bd) + _dot(lo, bd)


def _rope128(x, cos, sin_signed, half):
    lane = lax.broadcasted_iota(jnp.int32, x.shape, 1)
    first = (lane & (2 * half - 1)) < half
    rot = jnp.where(first, pltpu.roll(x, LANES - half, 1), pltpu.roll(x, half, 1))
    return x * cos + rot * sin_signed


def _sigmoid(x):
    return 1.0 / (1.0 + jnp.exp(-x))


def _mix_in_kernel(hs_ref, g_ref, wa_ref, wb_ref, wba_ref, wc_ref, aqn_ref, akn_ref, cqn_ref, ckn_ref,
                   cosa_ref, sina_ref, cosc_ref, sinc_ref, bd64_ref, bd32_ref,
                   qa_ref, ka_ref, va_ref, bx_ref, bz_ref, ba_ref, cq_ref, ck_ref, cv_ref, *, tm):
    j = pl.program_id(1)
    x = hs_ref[...]
    ms = jnp.mean(x * x, axis=-1, keepdims=True)
    h = x * lax.rsqrt(ms + EPS) * g_ref[...]
    row = j * tm + lax.broadcasted_iota(jnp.int32, (tm, 1), 0)
    h = jnp.where(row >= PAD_LEN, h, 0.0)
    hb = h.astype(BF16)

    cosa, sina = cosa_ref[...], sina_ref[...]
    cosc, sinc = cosc_ref[...], sinc_ref[...]
    bd64, bd32 = bd64_ref[...], bd32_ref[...]

    def norm_rope(p, gain, bd, cos, sin, half):
        y = p * lax.rsqrt(_group_mean(p * p, bd) + EPS) * gain
        return _rope128(y, cos, sin, half)

    for c in range(A_HEADS):
        p = _dot(hb, wa_ref[:, c * LANES:(c + 1) * LANES])
        qa_ref[:, c * LANES:(c + 1) * LANES] = norm_rope(
            p, aqn_ref[:, c * LANES:(c + 1) * LANES], bd64, cosa, sina, A_HEAD_DIM // 2).astype(BF16)
    p = _dot(hb, wa_ref[:, 4 * LANES:5 * LANES])
    ka_ref[...] = norm_rope(p, akn_ref[...], bd64, cosa, sina, A_HEAD_DIM // 2).astype(BF16)
    va_ref[...] = _dot(hb, wa_ref[:, 5 * LANES:6 * LANES]).astype(BF16)

    for c in range(3 * B_WIDTH // 512):
        bx_ref[:, c * 512:(c + 1) * 512] = _dot(hb, wb_ref[:, c * 512:(c + 1) * 512])
    bz_ref[...] = _dot(hb, wb_ref[:, 3 * B_WIDTH:4 * B_WIDTH])
    ba_ref[...] = _dot(hb, wba_ref[...])

    for c in range(2):
        p = _dot(hb, wc_ref[:, c * LANES:(c + 1) * LANES])
        cq_ref[:, c * LANES:(c + 1) * LANES] = norm_rope(
            p, cqn_ref[...], bd32, cosc, sinc, C_HEAD_DIM // 2).astype(BF16)
        p = _dot(hb, wc_ref[:, (2 + c) * LANES:(3 + c) * LANES])
        ck_ref[:, c * LANES:(c + 1) * LANES] = norm_rope(
            p, ckn_ref[...], bd32, cosc, sinc, C_HEAD_DIM // 2).astype(BF16)
    cv_ref[...] = _dot(hb, wc_ref[:, 4 * LANES:6 * LANES]).astype(BF16)


def _mix_in(hs, gain, wa, wb, wba, wc, aqn, akn, cqn, ckn, tabs, bd64, bd32, *, bsz, seqlen, tm):
    nt = seqlen // tm
    n = bsz * seqlen
    row = lambda w: pl.BlockSpec((tm, w), lambda b, j: (b * nt + j, 0))
    full = lambda a: pl.BlockSpec(a.shape, lambda b, j: (0, 0))
    tab = pl.BlockSpec((tm, LANES), lambda b, j: (j, 0))
    cosa, sina, cosc, sinc = tabs
    out_shapes = [
        jax.ShapeDtypeStruct((n, 4 * LANES), BF16),
        jax.ShapeDtypeStruct((n, LANES), BF16),
        jax.ShapeDtypeStruct((n, LANES), BF16),
        jax.ShapeDtypeStruct((n, 3 * B_WIDTH), F32),
        jax.ShapeDtypeStruct((n, B_WIDTH), F32),
        jax.ShapeDtypeStruct((n, LANES), F32),
        jax.ShapeDtypeStruct((n, C_WIDTH), BF16),
        jax.ShapeDtypeStruct((n, C_WIDTH), BF16),
        jax.ShapeDtypeStruct((n, C_WIDTH), BF16),
    ]
    return pl.pallas_call(
        functools.partial(_mix_in_kernel, tm=tm),
        grid=(bsz, nt),
        in_specs=[row(D_MODEL), full(gain), full(wa), full(wb), full(wba), full(wc), full(aqn), full(akn),
                  full(cqn), full(ckn), tab, tab, tab, tab, full(bd64), full(bd32)],
        out_specs=[row(s.shape[1]) for s in out_shapes],
        out_shape=out_shapes,
        compiler_params=_cparams(2),
        name="mix_in",
    )(hs, gain, wa, wb, wba, wc, aqn, akn, cqn, ckn, cosa, sina, cosc, sinc, bd64, bd32)


def _attn_a_kernel(sink_ref, q_ref, k_ref, v_ref, o_ref, k0_ref, v0_ref, kp_ref, vp_ref):
    n = pl.program_id(1)
    kc = k_ref[...]
    vc = v_ref[...]

    @pl.when(n == 0)
    def _():
        k0_ref[...] = kc
        v0_ref[...] = vc
        kp_ref[...] = jnp.zeros_like(kp_ref)
        vp_ref[...] = jnp.zeros_like(vp_ref)

    kcat = jnp.concatenate([k0_ref[...], kp_ref[...], kc], axis=0)
    vcat = jnp.concatenate([v0_ref[...], vp_ref[...], vc], axis=0)

    qi = lax.broadcasted_iota(jnp.int32, (2 * BLOCK, 3 * BLOCK), 0) & (BLOCK - 1)
    kj = lax.broadcasted_iota(jnp.int32, (2 * BLOCK, 3 * BLOCK), 1)
    qpos = n * BLOCK + qi
    part = kj >> 7
    j = kj & (BLOCK - 1)
    off_prev = jnp.where(n >= 2, 0, BLOCK)
    off_cur = jnp.where(n >= 1, 0, BLOCK)
    lo = jnp.where(part == 0, PAD_LEN, jnp.where(part == 1, qi + 1 + off_prev, 0))
    hi = jnp.where(part == 0, qpos, jnp.where(part == 1, BLOCK, qi - off_cur))
    ok = (j >= lo) & (j <= hi)

    for g in range(A_KV_HEADS):
        q2 = jnp.concatenate([q_ref[:, (2 * g) * LANES:(2 * g + 1) * LANES],
                              q_ref[:, (2 * g + 1) * LANES:(2 * g + 2) * LANES]], axis=0)
        s = _dot_nt(q2, kcat)
        s = jnp.where(ok, s, NEG_INF)
        hrow = lax.broadcasted_iota(jnp.int32, (2 * BLOCK, 1), 0) >> 7
        sink = jnp.where(hrow == 0, sink_ref[2 * g], sink_ref[2 * g + 1])
        m = jnp.maximum(jnp.max(s, axis=-1, keepdims=True), sink)
        p = jnp.exp(s - m)
        l = jnp.sum(p, axis=-1, keepdims=True) + jnp.exp(sink - m)
        o = _dot(p.astype(BF16), vcat) / l
        o_ref[:, (2 * g) * LANES:(2 * g + 1) * LANES] = o[:BLOCK].astype(BF16)
        o_ref[:, (2 * g + 1) * LANES:(2 * g + 2) * LANES] = o[BLOCK:].astype(BF16)

    kp_ref[...] = kc
    vp_ref[...] = vc


def _attn_a(sinks, qa, ka, va, *, bsz, seqlen):
    nb = seqlen // BLOCK
    n = bsz * seqlen
    blk = lambda w: pl.BlockSpec((BLOCK, w), lambda b, i: (b * nb + i, 0))
    return pl.pallas_call(
        _attn_a_kernel,
        grid=(bsz, nb),
        in_specs=[pl.BlockSpec(memory_space=pltpu.SMEM), blk(4 * LANES), blk(LANES), blk(LANES)],
        out_specs=blk(4 * LANES),
        out_shape=jax.ShapeDtypeStruct((n, 4 * LANES), BF16),
        scratch_shapes=[pltpu.VMEM((BLOCK, LANES), BF16)] * 4,
        compiler_params=_cparams(2),
        name="attn_a",
    )(sinks, qa, ka, va)


def _gdn_kernel(x_ref, z_ref, ba_ref, cw_ref, alog_ref, dtb_ref, on_ref, o_ref,
                xbuf, s_ref, *, tile):
    t = pl.program_id(1)
    nchunk = tile // CHUNK
    dk = B_HEAD_DIM

    @pl.when(t == 0)
    def _():
        xbuf[0:8, :] = jnp.zeros((8, 3 * B_WIDTH), F32)
        s_ref[...] = jnp.zeros_like(s_ref)

    xbuf[8:8 + tile, :] = x_ref[...]

    ri = lax.broadcasted_iota(jnp.int32, (CHUNK, CHUNK), 0)
    ci = lax.broadcasted_iota(jnp.int32, (CHUNK, CHUNK), 1)
    incl = ri >= ci
    strict = ri > ci
    lmat = jnp.concatenate([incl.astype(BF16), jnp.full((CHUNK, CHUNK), -1.0, BF16)], axis=1)
    umask = ri <= ci
    eye = (ri == ci).astype(F32)
    alog = alog_ref[...]
    dtb = dtb_ref[...]
    onorm = on_ref[...]
    cw = cw_ref[...]

    def chunk_body(c, carry):
        r0 = pl.multiple_of(c * CHUNK, CHUNK)
        xw = xbuf[pl.ds(r0, CHUNK + 8), :]
        y = cw[0:1, :] * xw[5:5 + CHUNK, :]
        for jj in range(1, CONV_WIDTH):
            y = y + cw[jj:jj + 1, :] * xw[5 + jj:5 + jj + CHUNK, :]
        y = y * _sigmoid(y)

        ba = ba_ref[pl.ds(r0, CHUNK), :]
        beta_all = _sigmoid(ba)
        xg = ba + dtb
        softplus = jnp.maximum(xg, 0.0) + jnp.log(1.0 + jnp.exp(-jnp.abs(xg)))
        g_all = -jnp.exp(alog) * softplus

        rowi = lax.broadcasted_iota(jnp.int32, (CHUNK, LANES), 0)
        gc_all = g_all
        sh = 1
        while sh < CHUNK:
            gc_all = gc_all + jnp.where(rowi >= sh, pltpu.roll(gc_all, sh, 0), 0.0)
            sh *= 2
        gl_all = gc_all[CHUNK - 1:CHUNK, :]

        z = z_ref[pl.ds(r0, CHUNK), :]

        for h in range(B_HEADS):
            q = y[:, h * dk:(h + 1) * dk]
            k = y[:, B_WIDTH + h * dk:B_WIDTH + (h + 1) * dk]
            v = y[:, 2 * B_WIDTH + h * dk:2 * B_WIDTH + (h + 1) * dk]
            q = q * lax.rsqrt(jnp.sum(q * q, axis=-1, keepdims=True) + L2_EPS) * (dk ** -0.5)
            k = k * lax.rsqrt(jnp.sum(k * k, axis=-1, keepdims=True) + L2_EPS)
            beta = beta_all[:, h:h + 1]
            g = g_all[:, 4 + h:5 + h]
            gc = gc_all[:, 4 + h:5 + h]
            gl = gl_all[:, 4 + h:5 + h]

            gd = jnp.broadcast_to(g, (CHUNK, CHUNK))
            rhs_m = jnp.concatenate([gd, jnp.where(umask, gd, 0.0)], axis=0)
            m_hi, m_mid, m_lo = _split3(rhs_m)
            mdiff = _dot(lmat, m_hi) + _dot(lmat, m_mid) + _dot(lmat, m_lo)
            decay = jnp.exp(jnp.where(incl, mdiff, NEG_INF))

            kb = (k * beta)
            kb16 = kb.astype(BF16)
            k16 = k.astype(BF16)
            q16 = q.astype(BF16)
            a_mat = jnp.where(strict, _dot_nt(kb16, k16) * decay, 0.0)
            qk = jnp.where(incl, _dot_nt(q16, k16) * decay, 0.0)

            xinv = eye - jnp.where((ri >> 1) == (ci >> 1), a_mat, 0.0)
            for lg in range(1, 6):
                sel = ((ri >> (lg + 1)) == (ci >> (lg + 1))) & ((ri >> lg) != (ci >> lg))
                qb = jnp.where(sel, a_mat, 0.0).astype(BF16)
                x16 = xinv.astype(BF16)
                xinv = xinv - _dot(x16, _dot(qb, x16).astype(BF16))

            egc = jnp.exp(gc)
            rhs = jnp.concatenate([v * beta, kb * egc], axis=1).astype(BF16)
            sol = _dot(xinv.astype(BF16), rhs)
            u = sol[:, :dk]
            w = sol[:, dk:]
            q_dec = (q * egc).astype(BF16)
            k_dec = (k * jnp.exp(gl - gc)).astype(BF16)

            state = s_ref[h]
            s16 = state.astype(BF16)
            v_new = u - _dot(w.astype(BF16), s16)
            vn16 = v_new.astype(BF16)
            o = _dot(q_dec, s16) + _dot(qk.astype(BF16), vn16)
            s_ref[h] = state * jnp.exp(gl) + _dot_tn(k_dec, vn16)

            on = o * lax.rsqrt(jnp.mean(o * o, axis=-1, keepdims=True) + EPS) * onorm
            zh = z[:, h * dk:(h + 1) * dk]
            o_ref[pl.ds(r0, CHUNK), h * dk:(h + 1) * dk] = (on * (zh * _sigmoid(zh))).astype(BF16)
        return carry

    lax.fori_loop(0, nchunk, chunk_body, 0)
    xbuf[0:8, :] = x_ref[tile - 8:tile, :]


def _gdn(bx, bz, ba, cw, alog, dtb, onorm, *, bsz, seqlen, tile):
    nt = seqlen // tile
    n = bsz * seqlen
    row = lambda w: pl.BlockSpec((tile, w), lambda b, j: (b * nt + j, 0))
    full = lambda a: pl.BlockSpec(a.shape, lambda b, j: (0, 0))
    return pl.pallas_call(
        functools.partial(_gdn_kernel, tile=tile),
        grid=(bsz, nt),
        in_specs=[row(3 * B_WIDTH), row(B_WIDTH), row(LANES), full(cw), full(alog), full(dtb), full(onorm)],
        out_specs=row(B_WIDTH),
        out_shape=jax.ShapeDtypeStruct((n, B_WIDTH), BF16),
        scratch_shapes=[pltpu.VMEM((tile + 8, 3 * B_WIDTH), F32),
                        pltpu.VMEM((B_HEADS, B_HEAD_DIM, B_HEAD_DIM), F32)],
        compiler_params=_cparams(2),
        name="gdn",
    )(bx, bz, ba, cw, alog, dtb, onorm)


def _attn_c_kernel(q_ref, k_ref, v_ref, lam_ref, sub_ref, bd64_ref, o_ref,
                   qm_ref, m_ref, l_ref, acc_ref, *, tq, tk, lambda_init):
    qi = pl.program_id(1)
    ki = pl.program_id(2)
    last_k = ((qi + 1) * tq - 1) // tk
    lane = lax.broadcasted_iota(jnp.int32, (tq, C_WIDTH), 1)

    @pl.when(ki == 0)
    def _():
        q = q_ref[...]
        for h in range(C_HEADS):
            for c in range(2):
                lo = h * 2 * C_HEAD_DIM + c * C_HEAD_DIM
                sel = (lane >= lo) & (lane < lo + C_HEAD_DIM)
                qm_ref[h, c * tq:(c + 1) * tq, :] = jnp.where(sel, q, jnp.zeros_like(q))
        m_ref[...] = jnp.full(m_ref.shape, NEG_INF, F32)
        l_ref[...] = jnp.zeros_like(l_ref)
        acc_ref[...] = jnp.zeros_like(acc_ref)

    @pl.when(ki <= last_k)
    def _():
        k = k_ref[...]
        v = v_ref[...]
        r = lax.broadcasted_iota(jnp.int32, (2 * tq, tk), 0)
        qpos = qi * tq + jnp.where(r >= tq, r - tq, r)
        kpos = ki * tk + lax.broadcasted_iota(jnp.int32, (2 * tq, tk), 1)
        ok = (kpos <= qpos) & (kpos >= PAD_LEN)
        for h in range(C_HEADS):
            s = _dot_nt(qm_ref[h], k)
            s = jnp.where(ok, s, NEG_INF)
            m_old = m_ref[h]
            m_new = jnp.maximum(m_old, jnp.max(s, axis=-1, keepdims=True))
            alpha = jnp.exp(m_old - m_new)
            p = jnp.exp(s - m_new)
            l_ref[h] = alpha * l_ref[h] + jnp.sum(p, axis=-1, keepdims=True)
            acc_ref[h] = alpha * acc_ref[h] + _dot(p.astype(BF16), v)
            m_ref[h] = m_new

    @pl.when(ki == last_k)
    def _():
        lp = lam_ref[...]
        lam = (jnp.exp(jnp.sum(lp[0:1] * lp[1:2], axis=-1, keepdims=True))
               - jnp.exp(jnp.sum(lp[2:3] * lp[3:4], axis=-1, keepdims=True)) + lambda_init)
        out = jnp.zeros((tq, C_WIDTH), F32)
        for h in range(C_HEADS):
            a = acc_ref[h] / l_ref[h]
            d = a[:tq] - lam * a[tq:]
            sel = (lane >= h * 2 * C_HEAD_DIM) & (lane < (h + 1) * 2 * C_HEAD_DIM)
            out = jnp.where(sel, d, out)
        y = out * lax.rsqrt(_group_mean(out * out, bd64_ref[...]) + EPS) * sub_ref[...]
        o_ref[...] = (y * (1.0 - lambda_init)).astype(BF16)


def _attn_c(cq, ck, cv, lam, sub, bd64, *, bsz, seqlen, tq, tk, lambda_init):
    nq = seqlen // tq
    nk = seqlen // tk
    n = bsz * seqlen

    def kv_map(b, i, j):
        return (b * nk + jnp.minimum(j, ((i + 1) * tq - 1) // tk), 0)

    qspec = pl.BlockSpec((tq, C_WIDTH), lambda b, i, j: (b * nq + i, 0))
    kvspec = pl.BlockSpec((tk, C_WIDTH), kv_map)
    full = lambda a: pl.BlockSpec(a.shape, lambda b, i, j: (0, 0))
    return pl.pallas_call(
        functools.partial(_attn_c_kernel, tq=tq, tk=tk, lambda_init=lambda_init),
        grid=(bsz, nq, nk),
        in_specs=[qspec, kvspec, kvspec, full(lam), full(sub), full(bd64)],
        out_specs=qspec,
        out_shape=jax.ShapeDtypeStruct((n, C_WIDTH), BF16),
        scratch_shapes=[pltpu.VMEM((C_HEADS, 2 * tq, C_WIDTH), BF16),
                        pltpu.VMEM((C_HEADS, 2 * tq, 1), F32),
                        pltpu.VMEM((C_HEADS, 2 * tq, 1), F32),
                        pltpu.VMEM((C_HEADS, 2 * tq, C_WIDTH), F32)],
        compiler_params=_cparams(3),
        name="attn_c",
    )(cq, ck, cv, lam, sub, bd64)


def _mix_out_kernel(hs_ref, oa_ref, ob_ref, oc_ref, wa_ref, wb_ref, wc_ref, g_ref, *rest, with_router):
    if with_router:
        rt_ref, hs_out_ref, h2_ref, ridx_ref, rw_ref = rest
    else:
        hs_out_ref, h2_ref = rest
    y = (hs_ref[...] + _dot(oa_ref[...], wa_ref[...]) + _dot(ob_ref[...], wb_ref[...])
         + _dot(oc_ref[...], wc_ref[...]))
    hs_out_ref[...] = y
    h = y * lax.rsqrt(jnp.mean(y * y, axis=-1, keepdims=True) + EPS) * g_ref[...]
    h2_ref[...] = h.astype(BF16)
    if with_router:
        h_hi, h_mid, _ = _split3(h)
        r = rt_ref[...]
        r_hi = r.astype(BF16)
        r_lo = (r - r_hi.astype(F32)).astype(BF16)
        logits = _dot(h_hi, r_hi) + _dot(h_mid, r_hi) + _dot(h_hi, r_lo)
        lane = lax.broadcasted_iota(jnp.int32, logits.shape, 1)
        logits = jnp.where(lane < N_EXPERTS, logits, -jnp.inf)
        m1 = jnp.max(logits, axis=-1, keepdims=True)
        i1 = jnp.min(jnp.where(logits == m1, lane, LANES), axis=-1, keepdims=True)
        l2 = jnp.where(lane == i1, -jnp.inf, logits)
        m2 = jnp.max(l2, axis=-1, keepdims=True)
        i2 = jnp.min(jnp.where(l2 == m2, lane, LANES), axis=-1, keepdims=True)
        e2 = jnp.exp(m2 - m1)
        w1 = 1.0 / (1.0 + e2)
        w2 = e2 / (1.0 + e2)
        ridx_ref[...] = jnp.where(lane == 0, i1, jnp.where(lane == 1, i2, 0))
        rw_ref[...] = jnp.where(lane == 0, w1, jnp.where(lane == 1, w2, 0.0))


def _mix_out(hs, oa, ob, oc, woa, wob, woc, gain, router, *, tm):
    n = hs.shape[0]
    row = lambda w: pl.BlockSpec((tm, w), lambda i: (i, 0))
    full = lambda a: pl.BlockSpec(a.shape, lambda i: (0, 0))
    with_router = router is not None
    ins = [hs, oa, ob, oc, woa, wob, woc, gain]
    in_specs = [row(D_MODEL), row(oa.shape[1]), row(ob.shape[1]), row(oc.shape[1]),
                full(woa), full(wob), full(woc), full(gain)]
    out_shapes = [jax.ShapeDtypeStruct((n, D_MODEL), F32), jax.ShapeDtypeStruct((n, D_MODEL), BF16)]
    if with_router:
        ins.append(router)
        in_specs.append(full(router))
        out_shapes += [jax.ShapeDtypeStruct((n, LANES), jnp.int32), jax.ShapeDtypeStruct((n, LANES), F32)]
    return pl.pallas_call(
        functools.partial(_mix_out_kernel, with_router=with_router),
        grid=(n // tm,),
        in_specs=in_specs,
        out_specs=[row(s.shape[1]) for s in out_shapes],
        out_shape=out_shapes,
        compiler_params=_cparams(1),
        name="mix_out_router" if with_router else "mix_out",
    )(*ins)


def _ffn_kernel(hs_ref, x_ref, wg_ref, wu_ref, wd_ref, o_ref):
    f = pl.program_id(1)

    @pl.when(f == 0)
    def _():
        o_ref[...] = hs_ref[...]

    x = x_ref[...]
    g = _dot(x, wg_ref[...])
    u = _dot(x, wu_ref[...])
    a = (g * _sigmoid(g) * u).astype(BF16)
    o_ref[...] += _dot(a, wd_ref[...])


def _ffn(hs, h2, wg, wu, wd, *, tm, tf):
    n = hs.shape[0]
    dff = wg.shape[1]
    return pl.pallas_call(
        _ffn_kernel,
        grid=(n // tm, dff // tf),
        in_specs=[pl.BlockSpec((tm, D_MODEL), lambda i, f: (i, 0)),
                  pl.BlockSpec((tm, D_MODEL), lambda i, f: (i, 0)),
                  pl.BlockSpec((D_MODEL, tf), lambda i, f: (0, f)),
                  pl.BlockSpec((D_MODEL, tf), lambda i, f: (0, f)),
                  pl.BlockSpec((tf, D_MODEL), lambda i, f: (f, 0))],
        out_specs=pl.BlockSpec((tm, D_MODEL), lambda i, f: (i, 0)),
        out_shape=jax.ShapeDtypeStruct((n, D_MODEL), F32),
        compiler_params=_cparams(2),
        name="ffn_dense",
    )(hs, h2, wg, wu, wd)


def _gmm_kernel(tile_ref, exp_ref, lo_ref, hi_ref, first_ref, x_ref, wg_ref, wu_ref, wd_ref, o_ref, *, tm):
    i = pl.program_id(0)
    f = pl.program_id(1)

    @pl.when((f == 0) & (first_ref[i] == 1))
    def _():
        o_ref[...] = jnp.zeros_like(o_ref)

    @pl.when(hi_ref[i] > lo_ref[i])
    def _():
        x = x_ref[...]
        g = _dot(x, wg_ref[0])
        u = _dot(x, wu_ref[0])
        row = tile_ref[i] * tm + lax.broadcasted_iota(jnp.int32, (tm, 1), 0)
        inside = (row >= lo_ref[i]) & (row < hi_ref[i])
        a = jnp.where(inside, g * _sigmoid(g) * u, 0.0).astype(BF16)
        o_ref[...] += _dot(a, wd_ref[0])


def _gmm(meta, xs, wg, wu, wd, *, tm, tf, n_items):
    n = xs.shape[0]
    dexp = wg.shape[2]
    grid_spec = pltpu.PrefetchScalarGridSpec(
        num_scalar_prefetch=5,
        grid=(n_items, dexp // tf),
        in_specs=[pl.BlockSpec((tm, D_MODEL), lambda i, f, tl, ex, lo, hi, fi: (tl[i], 0)),
                  pl.BlockSpec((1, D_MODEL, tf), lambda i, f, tl, ex, lo, hi, fi: (ex[i], 0, f)),
                  pl.BlockSpec((1, D_MODEL, tf), lambda i, f, tl, ex, lo, hi, fi: (ex[i], 0, f)),
                  pl.BlockSpec((1, tf, D_MODEL), lambda i, f, tl, ex, lo, hi, fi: (ex[i], f, 0))],
        out_specs=pl.BlockSpec((tm, D_MODEL), lambda i, f, tl, ex, lo, hi, fi: (tl[i], 0)),
    )
    return pl.pallas_call(
        functools.partial(_gmm_kernel, tm=tm),
        grid_spec=grid_spec,
        out_shape=jax.ShapeDtypeStruct((n, D_MODEL), F32),
        compiler_params=_cparams(2),
        name="ffn_moe_grouped",
    )(*meta, xs, wg, wu, wd)


def _combine_kernel(hs_ref, y0_ref, y1_ref, w_ref, o_ref):
    w = w_ref[...]
    o_ref[...] = hs_ref[...] + w[:, 0:1] * y0_ref[...] + w[:, 1:2] * y1_ref[...]


def _combine(hs, y0, y1, rw, *, tm):
    n = hs.shape[0]
    row = lambda w: pl.BlockSpec((tm, w), lambda i: (i, 0))
    return pl.pallas_call(
        _combine_kernel,
        grid=(n // tm,),
        in_specs=[row(D_MODEL), row(D_MODEL), row(D_MODEL), row(LANES)],
        out_specs=row(D_MODEL),
        out_shape=jax.ShapeDtypeStruct((n, D_MODEL), F32),
        compiler_params=_cparams(1),
        name="moe_combine",
    )(hs, y0, y1, rw)


def _moe_layer(hs, h2, ridx, rw, wg, wu, wd, *, tm):
    n = hs.shape[0]
    n_assign = TOP_K * n
    n_tiles = n_assign // tm
    n_items = n_tiles + N_EXPERTS - 1
    e2 = ridx[:, :TOP_K]
    onehot = (e2[:, :, None] == jnp.arange(N_EXPERTS)[None, None, :]).astype(jnp.int32).sum(axis=1)
    csum = jnp.cumsum(onehot, axis=0)
    counts = csum[-1]
    rank = csum - onehot
    ends = jnp.cumsum(counts)
    offs = ends - counts
    pos = jnp.take_along_axis(offs[None, :] + rank, e2, axis=1)
    first_tile = offs // tm
    last_tile = jnp.maximum(ends - 1, 0) // tm
    ntile = jnp.where(counts > 0, last_tile - first_tile + 1, 0)
    item_end = jnp.cumsum(ntile)
    item_start = item_end - ntile
    total = item_end[-1]
    idx = jnp.arange(n_items)
    ex = jnp.minimum(jnp.searchsorted(item_end, idx, side="right"), N_EXPERTS - 1).astype(jnp.int32)
    tl = first_tile[ex] + (idx - item_start[ex])
    valid = idx < total
    last_valid = jnp.maximum(total - 1, 0)
    ex = jnp.where(valid, ex, ex[last_valid]).astype(jnp.int32)
    tl = jnp.where(valid, tl, tl[last_valid]).astype(jnp.int32)
    lo = jnp.where(valid, offs[ex], 0).astype(jnp.int32)
    hi = jnp.where(valid, ends[ex], 0).astype(jnp.int32)
    first = (valid & ((idx == 0) | (tl != jnp.roll(tl, 1)))).astype(jnp.int32)
    src = jnp.zeros((n_assign,), jnp.int32).at[pos.reshape(-1)].set(
        jnp.repeat(jnp.arange(n, dtype=jnp.int32), TOP_K))
    xs = jnp.take(h2, src, axis=0)
    ys = _gmm((tl, ex, lo, hi, first), xs, wg, wu, wd, tm=tm, tf=512, n_items=n_items)
    y0 = jnp.take(ys, pos[:, 0], axis=0)
    y1 = jnp.take(ys, pos[:, 1], axis=0)
    return _combine(hs, y0, y1, rw, tm=512)


def _rope_tables(seqlen, head_dim):
    half = head_dim // 2
    pos = (jnp.arange(seqlen) - PAD_LEN).astype(F32)
    inv_freq = ROPE_THETA ** (-jnp.arange(half, dtype=F32) * (2.0 / head_dim))
    ang = pos[:, None] * inv_freq[None, :]
    cos, sin = jnp.cos(ang), jnp.sin(ang)
    reps = LANES // head_dim
    cos_t = jnp.tile(jnp.concatenate([cos, cos], axis=1), (1, reps))
    sin_t = jnp.tile(jnp.concatenate([-sin, sin], axis=1), (1, reps))
    return cos_t, sin_t


def _block_diag_mean(width, group):
    i = jnp.arange(width)
    return jnp.where((i[:, None] // group) == (i[None, :] // group), 1.0 / group, 0.0).astype(BF16)


def _forward(x, meta_tokens, mix_norm, w_in, a_q_norm, a_k_norm, a_sinks, b_conv, b_a_log, b_dt_bias, b_o_norm,
             c_q_norm, c_k_norm, c_lambda, c_subln, w_out, ffn_norm, dense_w_gate, dense_w_up, dense_w_down,
             router, moe_w_gate, moe_w_up, moe_w_down, *, tm_in, gdn_tile, tq, tk, tm_out, tm_ffn, tm_moe):
    bsz, seq, dm = x.shape
    depth = w_in.shape[0]
    seqlen = PREFIX + seq
    n = bsz * seqlen
    pad = jnp.zeros((bsz, PAD_LEN, dm), x.dtype)
    meta = jnp.broadcast_to(meta_tokens.astype(x.dtype)[None], (bsz, N_META, dm))
    hs = jnp.concatenate([pad, meta, x], axis=1).reshape(n, dm)

    tabs = _rope_tables(seqlen, A_HEAD_DIM) + _rope_tables(seqlen, C_HEAD_DIM)
    bd64 = _block_diag_mean(LANES, A_HEAD_DIM)
    bd32 = _block_diag_mean(LANES, C_HEAD_DIM)
    bd64c = _block_diag_mean(C_WIDTH, 2 * C_HEAD_DIM)
    zeros64 = jnp.zeros((dm, A_HEAD_DIM), F32)

    for layer in range(depth):
        w = w_in[layer]
        qcols = []
        for h in range(A_HEADS):
            wq = w[:, h * A_HEAD_DIM:(h + 1) * A_HEAD_DIM]
            qcols += [wq, zeros64] if (h // 2) == 0 else [zeros64, wq]
        wa = jnp.concatenate(qcols + [w[:, A_WIDTH:A_WIDTH + 2 * LANES]], axis=1).astype(BF16)
        o_b = A_WIDTH + 2 * A_KV_HEADS * A_HEAD_DIM
        wb = w[:, o_b:o_b + 4 * B_WIDTH].astype(BF16)
        o_ba = o_b + 4 * B_WIDTH
        wba = jnp.pad(w[:, o_ba:o_ba + 2 * B_HEADS], ((0, 0), (0, LANES - 2 * B_HEADS))).astype(BF16)
        o_c = o_ba + 2 * B_HEADS
        wc = w[:, o_c:o_c + 3 * C_WIDTH].astype(BF16)

        a_scale = A_HEAD_DIM ** -0.5
        aq = a_q_norm[layer] * a_scale
        z64 = jnp.zeros((A_HEAD_DIM,), F32)
        aqn = jnp.concatenate([jnp.concatenate([aq, z64]) if (h // 2) == 0 else jnp.concatenate([z64, aq])
                               for h in range(A_HEADS)])[None, :]
        akn = jnp.tile(a_k_norm[layer], LANES // A_HEAD_DIM)[None, :]
        cqn = jnp.tile(c_q_norm[layer] * (C_HEAD_DIM ** -0.5), LANES // C_HEAD_DIM)[None, :]
        ckn = jnp.tile(c_k_norm[layer], LANES // C_HEAD_DIM)[None, :]

        qa, ka, va, bx, bz, ba, cq, ck, cv = _mix_in(
            hs, mix_norm[layer][None, :], wa, wb, wba, wc, aqn, akn, cqn, ckn, tabs, bd64, bd32,
            bsz=bsz, seqlen=seqlen, tm=tm_in)

        oa = _attn_a(a_sinks[layer], qa, ka, va, bsz=bsz, seqlen=seqlen)

        lane_pad = lambda v4: jnp.pad(v4, (B_HEADS, LANES - 2 * B_HEADS))[None, :]
        ob = _gdn(bx, bz, ba, b_conv[layer], lane_pad(b_a_log[layer]), lane_pad(b_dt_bias[layer]),
                  b_o_norm[layer][None, :], bsz=bsz, seqlen=seqlen, tile=gdn_tile)

        lambda_init = 0.8 - 0.6 * math.exp(-0.3 * layer)
        lam_in = jnp.pad(c_lambda[layer], ((0, 4), (0, LANES - C_HEAD_DIM)))
        sub = jnp.tile(c_subln[layer], C_HEADS)[None, :]
        oc = _attn_c(cq, ck, cv, lam_in, sub, bd64c, bsz=bsz, seqlen=seqlen, tq=tq, tk=tk,
                     lambda_init=lambda_init)

        wo = w_out[layer]
        z_rows = jnp.zeros((A_HEAD_DIM, dm), F32)
        rows = []
        for h in range(A_HEADS):
            wr = wo[h * A_HEAD_DIM:(h + 1) * A_HEAD_DIM]
            rows += [wr, z_rows] if (h // 2) == 0 else [z_rows, wr]
        woa = jnp.concatenate(rows, axis=0).astype(BF16)
        wob = wo[A_WIDTH:A_WIDTH + B_WIDTH].astype(BF16)
        woc = wo[A_WIDTH + B_WIDTH:].astype(BF16)
        jj = layer // 2
        if layer % 2 == 0:
            hs, h2 = _mix_out(hs, oa, ob, oc, woa, wob, woc, ffn_norm[layer][None, :], None, tm=tm_out)
            hs = _ffn(hs, h2, dense_w_gate[jj].astype(BF16), dense_w_up[jj].astype(BF16),
                      dense_w_down[jj].astype(BF16), tm=tm_ffn, tf=256)
        else:
            rt = jnp.pad(router[jj], ((0, 0), (0, LANES - N_EXPERTS)))
            hs, h2, ridx, rw = _mix_out(hs, oa, ob, oc, woa, wob, woc, ffn_norm[layer][None, :], rt, tm=tm_out)
            hs = _moe_layer(hs, h2, ridx, rw, moe_w_gate[jj].astype(BF16), moe_w_up[jj].astype(BF16),
                            moe_w_down[jj].astype(BF16), tm=tm_moe)
    return hs.reshape(bsz, seqlen, dm)[:, PREFIX:]


def kernel(x, meta_tokens, mix_norm, w_in, a_q_norm, a_k_norm, a_sinks, b_conv, b_a_log, b_dt_bias, b_o_norm, c_q_norm, c_k_norm, c_lambda, c_subln, w_out, ffn_norm, dense_w_gate, dense_w_up, dense_w_down, router, moe_w_gate, moe_w_up, moe_w_down):
    return _forward(x, meta_tokens, mix_norm, w_in, a_q_norm, a_k_norm, a_sinks, b_conv, b_a_log, b_dt_bias,
                    b_o_norm, c_q_norm, c_k_norm, c_lambda, c_subln, w_out, ffn_norm, dense_w_gate, dense_w_up,
                    dense_w_down, router, moe_w_gate, moe_w_up, moe_w_down,
                    tm_in=640, gdn_tile=640, tq=640, tk=640, tm_out=512, tm_ffn=640, tm_moe=1024)
```
